```python
import jax, jax.numpy as jnp
from jax import lax
import numpy as np

D_MODEL = 1024
BATCH = 4
SEQ = 4096
DEPTH = 1

MEM_LEN = 256
RET_WIDTH = D_MODEL // 2
RET_HEADS = 4
RET_DK = RET_WIDTH // RET_HEADS
RET_DV = RET_WIDTH // RET_HEADS
SB_WIDTH = D_MODEL - RET_WIDTH
SB_HEADS = 8
SB_DH = SB_WIDTH // SB_HEADS
MIX_WIDTH = RET_WIDTH + SB_WIDTH
IN_COLS = 4 * RET_WIDTH + 3 * SB_WIDTH
X_HEADS = 4
X_DH = D_MODEL // X_HEADS
D_FF = 4 * D_MODEL
CHUNK = 128
Q_BLOCK = 128
ROPE_BASE = 10000.0
EPS = 1e-6

kernel_name = "hybrid_retention_stickbreaking_block"


def rms_norm(t, gain):
    tf = t.astype(jnp.float32)
    return tf * lax.rsqrt(jnp.mean(tf * tf, axis=-1, keepdims=True) + EPS) * gain.astype(jnp.float32)


def split_heads(t, n_heads):
    b, s, _ = t.shape
    return t.reshape(b, s, n_heads, -1).transpose(0, 2, 1, 3)


def merge_heads(t):
    b, h, s, d = t.shape
    return t.transpose(0, 2, 1, 3).reshape(b, s, h * d)


def rotary(t, positions):
    d = t.shape[-1]
    inv_freq = ROPE_BASE ** (-jnp.arange(0, d, 2, dtype=jnp.float32) / d)
    ang = positions.astype(jnp.float32)[:, None, :, None] * inv_freq
    cos, sin = jnp.cos(ang), jnp.sin(ang)
    tf = t.astype(jnp.float32)
    t1, t2 = tf[..., : d // 2], tf[..., d // 2:]
    return jnp.concatenate([t1 * cos - t2 * sin, t2 * cos + t1 * sin], axis=-1)


def chunkwise_retention(q, k, v):
    b, h, s, dk = q.shape
    dv = v.shape[-1]
    n_chunks = s // CHUNK
    log_gamma = jnp.log1p(-(2.0 ** (-5.0 - jnp.arange(h, dtype=jnp.float32))))
    idx = jnp.arange(CHUNK, dtype=jnp.float32)
    rel = idx[:, None] - idx[None, :]
    causal = rel >= 0
    decay_in = jnp.where(causal, jnp.exp(log_gamma[:, None, None] * jnp.where(causal, rel, 0.0)), 0.0)
    q_decay = jnp.exp(log_gamma[:, None] * (idx + 1.0))
    k_decay = jnp.exp(log_gamma[:, None] * (CHUNK - 1.0 - idx))
    chunk_decay = jnp.exp(log_gamma * CHUNK)

    def to_chunks(t):
        return t.astype(jnp.float32).reshape(b, h, n_chunks, CHUNK, -1).transpose(2, 0, 1, 3, 4)

    qc, kc, vc = to_chunks(q), to_chunks(k), to_chunks(v)

    def step(state, inp):
        qi, ki, vi = inp
        scores = jnp.einsum('bhqd,bhkd->bhqk', qi, ki) * decay_in
        out = (jnp.einsum('bhqk,bhkv->bhqv', scores, vi)
               + jnp.einsum('bhqd,bhdv->bhqv', qi * q_decay[None, :, :, None], state))
        state = (state * chunk_decay[None, :, None, None]
                 + jnp.einsum('bhkd,bhkv->bhdv', ki * k_decay[None, :, :, None], vi))
        return state, out

    state0 = jnp.zeros((b, h, dk, dv), jnp.float32)
    _, out = lax.scan(step, state0, (qc, kc, vc))
    return out.transpose(1, 2, 0, 3, 4).reshape(b, h, s, dv)


def stick_breaking_attention(q, k, v):
    b, h, s, d = q.shape
    scale = d ** -0.5
    n_blocks = s // Q_BLOCK
    qb = q.astype(jnp.float32).reshape(b, h, n_blocks, Q_BLOCK, d).transpose(2, 0, 1, 3, 4)
    kf = k.astype(jnp.float32)
    vf = v.astype(jnp.float32)
    kpos = jnp.arange(s)

    def block(args):
        qi, start = args
        z = jnp.einsum('bhqd,bhkd->bhqk', qi, kf) * scale
        qpos = start + jnp.arange(Q_BLOCK)
        mask = kpos[None, :] < qpos[:, None]
        log_beta = jax.nn.log_sigmoid(z)
        log_one_minus = jnp.where(mask, jax.nn.log_sigmoid(-z), 0.0)
        tail = lax.cumsum(log_one_minus, axis=3, reverse=True) - log_one_minus
        w = jnp.where(mask, jnp.exp(log_beta + tail), 0.0)
        return jnp.einsum('bhqk,bhkd->bhqd', w, vf)

    starts = jnp.arange(n_blocks, dtype=jnp.int32) * Q_BLOCK
    out = lax.map(block, (qb, starts))
    return out.transpose(1, 2, 0, 3, 4).reshape(b, h, s, d)


def setup_inputs(seed: int = 0) -> dict:
    key = jax.random.key(seed)
    ks = jax.random.split(key, 20)

    def w(k, shape, fan_in):
        return jax.random.normal(k, shape, jnp.float32) * (fan_in ** -0.5)

    def gain(k, shape):
        return 1.0 + 0.02 * jax.random.normal(k, shape, jnp.float32)

    x = jax.random.normal(ks[0], (BATCH, SEQ, D_MODEL), jnp.float32)
    mem = jax.random.normal(ks[1], (BATCH, MEM_LEN, D_MODEL), jnp.float32)
    offset = jax.random.randint(ks[2], (BATCH,), 0, 1024, dtype=jnp.int32)
    positions = (offset[:, None] + jnp.arange(SEQ, dtype=jnp.int32)[None, :]).astype(jnp.int32)
    return {
        "x": x,
        "mem": mem,
        "positions": positions,
        "g_mix": gain(ks[3], (DEPTH, D_MODEL)),
        "w_in": w(ks[4], (DEPTH, D_MODEL, IN_COLS), D_MODEL),
        "ret_gn_g": gain(ks[5], (DEPTH, RET_HEADS, RET_DV)),
        "sb_q_g": gain(ks[6], (DEPTH, SB_HEADS, SB_DH)),
        "sb_k_g": gain(ks[7], (DEPTH, SB_HEADS, SB_DH)),
        "w_out": w(ks[8], (DEPTH, MIX_WIDTH, D_MODEL), MIX_WIDTH),
        "g_xattn": gain(ks[9], (DEPTH, D_MODEL)),
        "g_mem": gain(ks[10], (DEPTH, D_MODEL)),
        "w_xq": w(ks[11], (DEPTH, D_MODEL, D_MODEL), D_MODEL),
        "w_xkv": w(ks[12], (DEPTH, D_MODEL, 2 * D_MODEL), D_MODEL),
        "xq_g": gain(ks[13], (DEPTH, X_HEADS, X_DH)),
        "xk_g": gain(ks[14], (DEPTH, X_HEADS, X_DH)),
        "w_xo": w(ks[15], (DEPTH, D_MODEL, D_MODEL), D_MODEL),
        "g_mlp": gain(ks[16], (DEPTH, D_MODEL)),
        "w_up": w(ks[17], (DEPTH, D_MODEL, D_FF), D_MODEL),
        "w_down": w(ks[18], (DEPTH, D_FF, D_MODEL), D_FF),
    }


def reference(x, mem, positions, g_mix, w_in, ret_gn_g, sb_q_g, sb_k_g, w_out,
              g_xattn, g_mem, w_xq, w_xkv, xq_g, xk_g, w_xo, g_mlp, w_up, w_down):
    R, S = RET_WIDTH, SB_WIDTH
    for layer in range(DEPTH):
        h = rms_norm(x, g_mix[layer])
        proj = h @ w_in[layer].astype(jnp.float32)
        rq, rk, rv, rg, sq, sk, sv = jnp.split(
            proj, [R, 2 * R, 3 * R, 4 * R, 4 * R + S, 4 * R + 2 * S], axis=-1)

        rq = rotary(split_heads(rq, RET_HEADS), positions)
        rk = rotary(split_heads(rk, RET_HEADS), positions) * (RET_DK ** -0.5)
        ro = chunkwise_retention(rq, rk, split_heads(rv, RET_HEADS))
        ro = rms_norm(ro, ret_gn_g[layer][None, :, None, :])
        ro = merge_heads(ro) * jax.nn.silu(rg)

        sq = rms_norm(split_heads(sq, SB_HEADS), sb_q_g[layer][None, :, None, :])
        sk = rms_norm(split_heads(sk, SB_HEADS), sb_k_g[layer][None, :, None, :])
        so = merge_heads(stick_breaking_attention(sq, sk, split_heads(sv, SB_HEADS)))

        mix = jnp.concatenate([ro, so], axis=-1)
        x = x + (mix @ w_out[layer].astype(jnp.float32)).astype(x.dtype)

        hx = rms_norm(x, g_xattn[layer])
        m = rms_norm(mem, g_mem[layer])
        xq = rms_norm(split_heads(hx @ w_xq[layer].astype(jnp.float32), X_HEADS),
                      xq_g[layer][None, :, None, :])
        xk, xv = jnp.split(m @ w_xkv[layer].astype(jnp.float32), 2, axis=-1)
        xk = rms_norm(split_heads(xk, X_HEADS), xk_g[layer][None, :, None, :])
        xv = split_heads(xv, X_HEADS)
        scores = jnp.einsum('bhqd,bhkd->bhqk', xq, xk) * (X_DH ** -0.5)
        probs = jax.nn.softmax(scores, axis=-1)
        xo = merge_heads(jnp.einsum('bhqk,bhkd->bhqd', probs, xv))
        x = x + (xo @ w_xo[layer].astype(jnp.float32)).astype(x.dtype)

        hm = rms_norm(x, g_mlp[layer])
        up = jnp.square(jax.nn.relu(hm @ w_up[layer].astype(jnp.float32)))
        x = x + (up @ w_down[layer].astype(jnp.float32)).astype(x.dtype)
    return x
```

```python
import functools
import math

import numpy as np
import jax
import jax.numpy as jnp
from jax import lax
from jax.experimental import pallas as pl
from jax.experimental.pallas import tpu as pltpu

D_MODEL = 1024
MEM_LEN = 256
RET_WIDTH = 512
RET_HEADS = 4
RET_DK = 128
SB_WIDTH = 512
SB_HEADS = 8
SB_DH = 64
IN_COLS = 4 * RET_WIDTH + 3 * SB_WIDTH
N_GROUPS = IN_COLS // 512
X_HEADS = 4
X_DH = 256
D_FF = 4 * D_MODEL
CHUNK = 128
ROPE_BASE = 10000.0
EPS = 1e-6

LANES = 128
TM_PROJ = 512
TS_RET = 512
TQ_SB = 256
TK_SB = 256
TM_POST = 512
FF_CHUNK = 1024

F32 = jnp.float32
BF16 = jnp.bfloat16
NT_DIMS = (((1,), (1,)), ((), ()))


def _dot(a, b):
    return jnp.dot(a, b, preferred_element_type=F32)


def _dot_nt(a, b):
    return lax.dot_general(a, b, NT_DIMS, preferred_element_type=F32)


def _rms(t, gain):
    ms = jnp.mean(t * t, axis=-1, keepdims=True)
    return t * lax.rsqrt(ms + EPS) * gain


def _resident(shape):
    nd = len(shape)
    return pl.BlockSpec(shape, lambda *_: (0,) * nd, pipeline_mode=pl.Buffered(1))


def _in_proj_kernel(x_ref, pos_ref, freq_ref, g_ref, w_ref, sbq_ref, sbk_ref, seg_ref,
                    main_ref, vt_ref):
    h = _rms(x_ref[...], g_ref[...]).astype(BF16)

    ang = pos_ref[...] * freq_ref[...]
    lane = lax.broadcasted_iota(jnp.int32, (1, LANES), 1)
    cos_f = jnp.cos(ang)
    sin_s = jnp.where(lane < RET_DK // 2, -1.0, 1.0) * jnp.sin(ang)

    def group(j):
        return _dot(h, w_ref[:, j * 512:(j + 1) * 512])

    def rotary(t, scale):
        parts = []
        for hd in range(RET_HEADS):
            th = t[:, hd * RET_DK:(hd + 1) * RET_DK]
            rot = pltpu.roll(th, RET_DK // 2, axis=1)
            parts.append((th * cos_f + rot * sin_s) * scale)
        return jnp.concatenate(parts, axis=-1)

    def head_norm(t, gain, scale):
        sq = t * t
        hi = sq.astype(BF16)
        lo = (sq - hi.astype(F32)).astype(BF16)
        ssum = _dot(hi, seg_ref[...]) + _dot(lo, seg_ref[...])
        return t * lax.rsqrt(ssum * (1.0 / SB_DH) + EPS) * (gain * scale)

    main_ref[0] = rotary(group(0), 1.0).astype(BF16)
    main_ref[1] = rotary(group(1), RET_DK ** -0.5).astype(BF16)
    main_ref[2] = group(2).astype(BF16)
    main_ref[3] = group(3).astype(BF16)
    main_ref[4] = head_norm(group(4), sbq_ref[...], SB_DH ** -0.5).astype(BF16)
    main_ref[5] = head_norm(group(5), sbk_ref[...], 1.0).astype(BF16)
    sv = group(6)
    for blk in range(TM_PROJ // TK_SB):
        vt_ref[blk] = sv[blk * TK_SB:(blk + 1) * TK_SB, :].T.astype(BF16)


def _in_proj(x2d, pos_b, freq, g_mix, w_in, sbq_g, sbk_g, seg, batch, seq):
    n = x2d.shape[0]
    nk_per_tile = TM_PROJ // TK_SB
    tiles_per_seq = seq // TM_PROJ
    return pl.pallas_call(
        _in_proj_kernel,
        grid=(n // TM_PROJ,),
        in_specs=[
            pl.BlockSpec((TM_PROJ, D_MODEL), lambda i: (i, 0)),
            pl.BlockSpec((TM_PROJ, LANES), lambda i: (i, 0)),
            _resident((1, LANES)),
            _resident((1, D_MODEL)),
            _resident((D_MODEL, IN_COLS)),
            _resident((1, SB_WIDTH)),
            _resident((1, SB_WIDTH)),
            _resident((SB_WIDTH, SB_WIDTH)),
        ],
        out_specs=[
            pl.BlockSpec((N_GROUPS - 1, TM_PROJ, 512), lambda i: (0, i, 0)),
            pl.BlockSpec((None, nk_per_tile, SB_WIDTH, TK_SB),
                         lambda i: (i // tiles_per_seq, i % tiles_per_seq, 0, 0)),
        ],
        out_shape=[
            jax.ShapeDtypeStruct((N_GROUPS - 1, n, 512), BF16),
            jax.ShapeDtypeStruct((batch, seq // TK_SB, SB_WIDTH, TK_SB), BF16),
        ],
        compiler_params=pltpu.CompilerParams(
            dimension_semantics=("arbitrary",), vmem_limit_bytes=48 * 1024 * 1024),
        name="in_proj",
    )(x2d, pos_b, freq, g_mix, w_in, sbq_g, sbk_g, seg)


def _retention_kernel(q_ref, k_ref, v_ref, g_ref, gn_ref, o_ref, state_ref):
    @pl.when(pl.program_id(1) == 0)
    def _():
        state_ref[...] = jnp.zeros_like(state_ref)

    row = lax.broadcasted_iota(jnp.int32, (CHUNK, CHUNK), 0).astype(F32)
    col = lax.broadcasted_iota(jnp.int32, (CHUNK, CHUNK), 1).astype(F32)
    rel = row - col
    log_gamma = [math.log1p(-(2.0 ** (-5.0 - hd))) for hd in range(RET_HEADS)]

    for hd in range(RET_HEADS):
        lg = log_gamma[hd]
        decay_in = jnp.where(rel >= 0, jnp.exp(lg * jnp.maximum(rel, 0.0)), 0.0)
        q_decay = jnp.exp(lg * (row + 1.0))
        k_decay = jnp.exp(lg * (CHUNK - 1.0 - row))
        chunk_decay = math.exp(lg * CHUNK)
        cols = slice(hd * RET_DK, (hd + 1) * RET_DK)
        gain = gn_ref[:, cols]
        for c in range(TS_RET // CHUNK):
            rows = slice(c * CHUNK, (c + 1) * CHUNK)
            q = q_ref[rows, cols]
            k = k_ref[rows, cols]
            v = v_ref[rows, cols]
            state = state_ref[hd]
            scores = _dot_nt(q, k) * decay_in
            out = _dot(scores.astype(BF16), v) + _dot(q, state.astype(BF16)) * q_decay
            kt = (k.astype(F32) * k_decay).T.astype(BF16)
            state_ref[hd] = state * chunk_decay + _dot(kt, v)
            gate = g_ref[rows, cols].astype(F32)
            gate = gate * (1.0 / (1.0 + jnp.exp(-gate)))
            o_ref[rows, cols] = (_rms(out, gain) * gate).astype(BF16)


def _retention(main, gn, batch, seq):
    n = batch * seq
    steps = seq // TS_RET

    def slot(s):
        return pl.BlockSpec((None, TS_RET, RET_WIDTH), lambda b, c: (s, b * steps + c, 0))

    return pl.pallas_call(
        _retention_kernel,
        grid=(batch, steps),
        in_specs=[slot(0), slot(1), slot(2), slot(3), _resident((1, RET_WIDTH))],
        out_specs=pl.BlockSpec((TS_RET, RET_WIDTH), lambda b, c: (b * steps + c, 0)),
        out_shape=jax.ShapeDtypeStruct((n, RET_WIDTH), BF16),
        scratch_shapes=[pltpu.VMEM((RET_HEADS, RET_DK, RET_DK), F32)],
        compiler_params=pltpu.CompilerParams(dimension_semantics=("arbitrary", "arbitrary")),
        name="retention",
    )(main, main, main, main, gn)


def _sb_kernel(q_ref, k_ref, vt_ref, o_ref):
    qi = pl.program_id(2)
    q = q_ref[...]
    lane = lax.broadcasted_iota(jnp.int32, (1, LANES), 1)
    krow = lax.broadcasted_iota(jnp.int32, (TK_SB, TQ_SB), 0)
    qcol = lax.broadcasted_iota(jnp.int32, (TK_SB, TQ_SB), 1)
    causal = krow < qcol
    later = (lax.broadcasted_iota(jnp.int32, (TK_SB, TK_SB), 1)
             > lax.broadcasted_iota(jnp.int32, (TK_SB, TK_SB), 0))
    upper = jnp.where(later, 1.0, 0.0).astype(BF16)
    upper2 = jnp.concatenate([upper, upper], axis=1)

    def block(j, carry, acc, q_head, diagonal):
        start = pl.multiple_of(j * TK_SB, TK_SB)
        z = _dot_nt(k_ref[pl.ds(start, TK_SB), :], q_head)
        log_beta = jnp.minimum(z, 0.0) - jnp.log(1.0 + jnp.exp(-jnp.abs(z)))
        log_om = log_beta - z
        if diagonal:
            log_om = jnp.where(causal, log_om, 0.0)
        hi = log_om.astype(BF16)
        lo = (log_om - hi.astype(F32)).astype(BF16)
        tail = _dot(upper2, jnp.concatenate([hi, lo], axis=0)) + carry
        w = jnp.exp(log_beta + tail)
        if diagonal:
            w = jnp.where(causal, w, 0.0)
        acc = acc + _dot(vt_ref[j], w.astype(BF16))
        carry = tail[0:1, :] + log_om[0:1, :]
        return carry, acc

    outs = []
    for head in range(2):
        own_lanes = (lane >= SB_DH) if head else (lane < SB_DH)
        q_head = jnp.where(own_lanes, q, jnp.zeros_like(q))
        carry = jnp.zeros((1, TQ_SB), F32)
        acc = jnp.zeros((LANES, TQ_SB), F32)
        carry, acc = block(qi, carry, acc, q_head, True)

        def body(t, ca, q_head=q_head):
            return block(qi - 1 - t, ca[0], ca[1], q_head, False)

        carry, acc = lax.fori_loop(0, qi, body, (carry, acc))
        outs.append(acc)
    vrow = lax.broadcasted_iota(jnp.int32, (LANES, 1), 0)
    out_t = jnp.where(vrow < SB_DH, outs[0], outs[1])
    o_ref[...] = out_t.T.astype(BF16)


def _stick_breaking(main, vt, batch, seq):
    n = batch * seq
    nq = seq // TQ_SB
    nk = seq // TK_SB
    pairs = SB_WIDTH // LANES
    return pl.pallas_call(
        _sb_kernel,
        grid=(batch, pairs, nq),
        in_specs=[
            pl.BlockSpec((None, TQ_SB, LANES), lambda b, p, i: (4, b * nq + i, p)),
            pl.BlockSpec((None, seq, LANES), lambda b, p, i: (5, b, p)),
            pl.BlockSpec((None, nk, LANES, TK_SB), lambda b, p, i: (b, 0, p, 0)),
        ],
        out_specs=pl.BlockSpec((TQ_SB, LANES), lambda b, p, i: (b * nq + i, p)),
        out_shape=jax.ShapeDtypeStruct((n, SB_WIDTH), BF16),
        compiler_params=pltpu.CompilerParams(
            dimension_semantics=("arbitrary", "arbitrary", "arbitrary")),
        name="stick_breaking",
    )(main, main, vt)


def _kv_proj_kernel(mem_ref, g_ref, w_ref, kg_ref, k_ref, v_ref):
    m = _rms(mem_ref[...], g_ref[...]).astype(BF16)
    kv = _dot(m, w_ref[...])
    for hd in range(X_HEADS):
        cols = slice(hd * X_DH, (hd + 1) * X_DH)
        k_ref[:, cols] = _rms(kv[:, cols], kg_ref[:, cols]).astype(BF16)
    v_ref[...] = kv[:, D_MODEL:].astype(BF16)


def _kv_proj(mem2d, g_mem, w_xkv, xk_g, batch):
    out = jax.ShapeDtypeStruct((batch * MEM_LEN, D_MODEL), BF16)
    return pl.pallas_call(
        _kv_proj_kernel,
        grid=(batch,),
        in_specs=[
            pl.BlockSpec((MEM_LEN, D_MODEL), lambda b: (b, 0)),
            _resident((1, D_MODEL)),
            _resident((D_MODEL, 2 * D_MODEL)),
            _resident((1, D_MODEL)),
        ],
        out_specs=[pl.BlockSpec((MEM_LEN, D_MODEL), lambda b: (b, 0))] * 2,
        out_shape=[out, out],
        compiler_params=pltpu.CompilerParams(dimension_semantics=("arbitrary",)),
        name="kv_proj",
    )(mem2d, g_mem, w_xkv, xk_g)


def _mix_xattn_kernel(x_ref, ro_ref, so_ref, wout_ref, gx_ref, wq_ref, qg_ref,
                      xk_ref, xv_ref, wo_ref, o_ref):
    x1 = (x_ref[...] + _dot(ro_ref[...], wout_ref[:RET_WIDTH, :])
          + _dot(so_ref[...], wout_ref[RET_WIDTH:, :]))
    hx = _rms(x1, gx_ref[...]).astype(BF16)
    xq = _dot(hx, wq_ref[...])
    heads = []
    for hd in range(X_HEADS):
        cols = slice(hd * X_DH, (hd + 1) * X_DH)
        qh = (_rms(xq[:, cols], qg_ref[:, cols]) * (X_DH ** -0.5)).astype(BF16)
        s = _dot_nt(qh, xk_ref[:, cols])
        p = jnp.exp(s - jnp.max(s, axis=-1, keepdims=True))
        denom = jnp.sum(p, axis=-1, keepdims=True)
        heads.append((_dot(p.astype(BF16), xv_ref[:, cols]) / denom).astype(BF16))
    xo = jnp.concatenate(heads, axis=-1)
    o_ref[...] = x1 + _dot(xo, wo_ref[...])


def _mix_xattn(x2d, ro, so, w_out, g_xattn, w_xq, xq_g, xk, xv, w_xo, seq):
    n = x2d.shape[0]
    tiles_per_seq = seq // TM_POST
    row = lambda i: (i, 0)
    per_batch = lambda i: (i // tiles_per_seq, 0)
    return pl.pallas_call(
        _mix_xattn_kernel,
        grid=(n // TM_POST,),
        in_specs=[
            pl.BlockSpec((TM_POST, D_MODEL), row),
            pl.BlockSpec((TM_POST, RET_WIDTH), row),
            pl.BlockSpec((TM_POST, SB_WIDTH), row),
            _resident((D_MODEL, D_MODEL)),
            _resident((1, D_MODEL)),
            _resident((D_MODEL, D_MODEL)),
            _resident((1, D_MODEL)),
            pl.BlockSpec((MEM_LEN, D_MODEL), per_batch),
            pl.BlockSpec((MEM_LEN, D_MODEL), per_batch),
            _resident((D_MODEL, D_MODEL)),
        ],
        out_specs=pl.BlockSpec((TM_POST, D_MODEL), row),
        out_shape=jax.ShapeDtypeStruct((n, D_MODEL), F32),
        compiler_params=pltpu.CompilerParams(
            dimension_semantics=("arbitrary",), vmem_limit_bytes=48 * 1024 * 1024),
        name="mix_xattn",
    )(x2d, ro, so, w_out, g_xattn, w_xq, xq_g, xk, xv, w_xo)


def _mlp_kernel(x_ref, g_ref, wup_ref, wdown_ref, o_ref):
    x = x_ref[...]
    hm = _rms(x, g_ref[...]).astype(BF16)
    acc = x
    for j in range(D_FF // FF_CHUNK):
        cols = slice(j * FF_CHUNK, (j + 1) * FF_CHUNK)
        up = jnp.maximum(_dot(hm, wup_ref[:, cols]), 0.0)
        acc = acc + _dot((up * up).astype(BF16), wdown_ref[cols, :])
    o_ref[...] = acc


def _mlp(x2d, g_mlp, w_up, w_down):
    n = x2d.shape[0]
    row = lambda i: (i, 0)
    return pl.pallas_call(
        _mlp_kernel,
        grid=(n // TM_POST,),
        in_specs=[
            pl.BlockSpec((TM_POST, D_MODEL), row),
            _resident((1, D_MODEL)),
            _resident((D_MODEL, D_FF)),
            _resident((D_FF, D_MODEL)),
        ],
        out_specs=pl.BlockSpec((TM_POST, D_MODEL), row),
        out_shape=jax.ShapeDtypeStruct((n, D_MODEL), F32),
        compiler_params=pltpu.CompilerParams(
            dimension_semantics=("arbitrary",), vmem_limit_bytes=48 * 1024 * 1024),
        name="mlp",
    )(x2d, g_mlp, w_up, w_down)


def kernel(x, mem, positions, g_mix, w_in, ret_gn_g, sb_q_g, sb_k_g, w_out, g_xattn, g_mem,
           w_xq, w_xkv, xq_g, xk_g, w_xo, g_mlp, w_up, w_down):
    batch, seq, _ = x.shape
    n = batch * seq
    depth = g_mix.shape[0]

    half = np.arange(LANES) % (RET_DK // 2)
    freq = jnp.asarray((ROPE_BASE ** (-(2.0 * half) / RET_DK)).reshape(1, LANES), F32)
    head_of = np.arange(SB_WIDTH) // SB_DH
    seg = jnp.asarray(head_of[:, None] == head_of[None, :], BF16)
    pos_b = jnp.broadcast_to(positions.astype(F32).reshape(n, 1), (n, LANES))

    x2d = x.reshape(n, D_MODEL)
    mem2d = mem.reshape(batch * MEM_LEN, D_MODEL)
    for layer in range(depth):
        main, vt = _in_proj(
            x2d, pos_b, freq, g_mix[layer].reshape(1, D_MODEL), w_in[layer].astype(BF16),
            sb_q_g[layer].reshape(1, SB_WIDTH), sb_k_g[layer].reshape(1, SB_WIDTH), seg,
            batch, seq)
        ro = _retention(main, ret_gn_g[layer].reshape(1, RET_WIDTH), batch, seq)
        so = _stick_breaking(main, vt, batch, seq)
        xk, xv = _kv_proj(mem2d, g_mem[layer].reshape(1, D_MODEL), w_xkv[layer].astype(BF16),
                          xk_g[layer].reshape(1, D_MODEL), batch)
        x2d = _mix_xattn(x2d, ro, so, w_out[layer].astype(BF16),
                         g_xattn[layer].reshape(1, D_MODEL), w_xq[layer].astype(BF16),
                         xq_g[layer].reshape(1, D_MODEL), xk, xv, w_xo[layer].astype(BF16), seq)
        x2d = _mlp(x2d, g_mlp[layer].reshape(1, D_MODEL), w_up[layer].astype(BF16),
                   w_down[layer].astype(BF16))
    return x2d.reshape(batch, seq, D_MODEL)
```

```python
import functools
import math

import numpy as np
import jax
import jax.numpy as jnp
from jax import lax
from jax.experimental import pallas as pl
from jax.experimental.pallas import tpu as pltpu

D_MODEL = 1024
MEM_LEN = 256
RET_WIDTH = 512
RET_HEADS = 4
RET_DK = 128
SB_WIDTH = 512
SB_HEADS = 8
SB_DH = 64
IN_COLS = 4 * RET_WIDTH + 3 * SB_WIDTH
N_GROUPS = IN_COLS // 512
X_HEADS = 4
X_DH = 256
D_FF = 4 * D_MODEL
CHUNK = 128
ROPE_BASE = 10000.0
EPS = 1e-6

LANES = 128
TM_PROJ = 512
TS_RET = 512
TQ_SB = 256
TK_SB = 256
TM_POST = 512
FF_CHUNK = 1024

F32 = jnp.float32
BF16 = jnp.bfloat16
NT_DIMS = (((1,), (1,)), ((), ()))
LOG2E = math.log2(math.e)
SB_SKIP_LOG2 = -160.0


def _dot(a, b):
    return jnp.dot(a, b, preferred_element_type=F32)


def _dot_nt(a, b):
    return lax.dot_general(a, b, NT_DIMS, preferred_element_type=F32)


def _rms(t, gain):
    ms = jnp.mean(t * t, axis=-1, keepdims=True)
    return t * lax.rsqrt(ms + EPS) * gain


def _resident(shape):
    nd = len(shape)
    return pl.BlockSpec(shape, lambda *_: (0,) * nd, pipeline_mode=pl.Buffered(1))


def _in_proj_kernel(x_ref, pos_ref, freq_ref, g_ref, w_ref, sbq_ref, sbk_ref, seg_ref,
                    main_ref, vt_ref):
    h = _rms(x_ref[...], g_ref[...]).astype(BF16)

    ang = pos_ref[...] * freq_ref[...]
    lane = lax.broadcasted_iota(jnp.int32, (1, LANES), 1)
    cos_f = jnp.cos(ang)
    sin_s = jnp.where(lane < RET_DK // 2, -1.0, 1.0) * jnp.sin(ang)

    def group(j):
        return _dot(h, w_ref[:, j * 512:(j + 1) * 512])

    def rotary(t, scale):
        parts = []
        for hd in range(RET_HEADS):
            th = t[:, hd * RET_DK:(hd + 1) * RET_DK]
            rot = pltpu.roll(th, RET_DK // 2, axis=1)
            parts.append((th * cos_f + rot * sin_s) * scale)
        return jnp.concatenate(parts, axis=-1)

    def head_norm(t, gain, scale):
        sq = t * t
        hi = sq.astype(BF16)
        lo = (sq - hi.astype(F32)).astype(BF16)
        ssum = _dot(hi, seg_ref[...]) + _dot(lo, seg_ref[...])
        return t * lax.rsqrt(ssum * (1.0 / SB_DH) + EPS) * (gain * scale)

    main_ref[0] = rotary(group(0), 1.0).astype(BF16)
    main_ref[1] = rotary(group(1), RET_DK ** -0.5).astype(BF16)
    main_ref[2] = group(2).astype(BF16)
    main_ref[3] = group(3).astype(BF16)
    main_ref[4] = head_norm(group(4), sbq_ref[...], LOG2E * SB_DH ** -0.5).astype(BF16)
    main_ref[5] = head_norm(group(5), sbk_ref[...], 1.0).astype(BF16)
    sv = group(6)
    for blk in range(TM_PROJ // TK_SB):
        vt_ref[blk] = sv[blk * TK_SB:(blk + 1) * TK_SB, :].T.astype(BF16)


def _in_proj(x2d, pos_b, freq, g_mix, w_in, sbq_g, sbk_g, seg, batch, seq):
    n = x2d.shape[0]
    nk_per_tile = TM_PROJ // TK_SB
    tiles_per_seq = seq // TM_PROJ
    return pl.pallas_call(
        _in_proj_kernel,
        grid=(n // TM_PROJ,),
        in_specs=[
            pl.BlockSpec((TM_PROJ, D_MODEL), lambda i: (i, 0)),
            pl.BlockSpec((TM_PROJ, LANES), lambda i: (i, 0)),
            _resident((1, LANES)),
            _resident((1, D_MODEL)),
            _resident((D_MODEL, IN_COLS)),
            _resident((1, SB_WIDTH)),
            _resident((1, SB_WIDTH)),
            _resident((SB_WIDTH, SB_WIDTH)),
        ],
        out_specs=[
            pl.BlockSpec((N_GROUPS - 1, TM_PROJ, 512), lambda i: (0, i, 0)),
            pl.BlockSpec((None, nk_per_tile, SB_WIDTH, TK_SB),
                         lambda i: (i // tiles_per_seq, i % tiles_per_seq, 0, 0)),
        ],
        out_shape=[
            jax.ShapeDtypeStruct((N_GROUPS - 1, n, 512), BF16),
            jax.ShapeDtypeStruct((batch, seq // TK_SB, SB_WIDTH, TK_SB), BF16),
        ],
        compiler_params=pltpu.CompilerParams(
            dimension_semantics=("arbitrary",), vmem_limit_bytes=48 * 1024 * 1024),
        name="in_proj",
    )(x2d, pos_b, freq, g_mix, w_in, sbq_g, sbk_g, seg)


def _retention_kernel(q_ref, k_ref, v_ref, g_ref, gn_ref, o_ref, state_ref):
    @pl.when(pl.program_id(1) == 0)
    def _():
        state_ref[...] = jnp.zeros_like(state_ref)

    row = lax.broadcasted_iota(jnp.int32, (CHUNK, CHUNK), 0).astype(F32)
    col = lax.broadcasted_iota(jnp.int32, (CHUNK, CHUNK), 1).astype(F32)
    rel = row - col
    log_gamma = [math.log1p(-(2.0 ** (-5.0 - hd))) for hd in range(RET_HEADS)]

    for hd in range(RET_HEADS):
        lg = log_gamma[hd]
        decay_in = jnp.where(rel >= 0, jnp.exp(lg * jnp.maximum(rel, 0.0)), 0.0)
        q_decay = jnp.exp(lg * (row + 1.0))
        k_decay = jnp.exp(lg * (CHUNK - 1.0 - row))
        chunk_decay = math.exp(lg * CHUNK)
        cols = slice(hd * RET_DK, (hd + 1) * RET_DK)
        gain = gn_ref[:, cols]
        for c in range(TS_RET // CHUNK):
            rows = slice(c * CHUNK, (c + 1) * CHUNK)
            q = q_ref[rows, cols]
            k = k_ref[rows, cols]
            v = v_ref[rows, cols]
            state = state_ref[hd]
            scores = _dot_nt(q, k) * decay_in
            out = _dot(scores.astype(BF16), v) + _dot(q, state.astype(BF16)) * q_decay
            kt = (k.astype(F32) * k_decay).T.astype(BF16)
            state_ref[hd] = state * chunk_decay + _dot(kt, v)
            gate = g_ref[rows, cols].astype(F32)
            gate = gate * (1.0 / (1.0 + jnp.exp(-gate)))
            o_ref[rows, cols] = (_rms(out, gain) * gate).astype(BF16)


def _retention(main, gn, batch, seq):
    n = batch * seq
    steps = seq // TS_RET

    def slot(s):
        return pl.BlockSpec((None, TS_RET, RET_WIDTH), lambda b, c: (s, b * steps + c, 0))

    return pl.pallas_call(
        _retention_kernel,
        grid=(batch, steps),
        in_specs=[slot(0), slot(1), slot(2), slot(3), _resident((1, RET_WIDTH))],
        out_specs=pl.BlockSpec((TS_RET, RET_WIDTH), lambda b, c: (b * steps + c, 0)),
        out_shape=jax.ShapeDtypeStruct((n, RET_WIDTH), BF16),
        scratch_shapes=[pltpu.VMEM((RET_HEADS, RET_DK, RET_DK), F32)],
        compiler_params=pltpu.CompilerParams(dimension_semantics=("arbitrary", "arbitrary")),
        name="retention",
    )(main, main, main, main, gn)


def _sb_scores(k_blk, q_head, upper2, causal):
    z = _dot_nt(k_blk, q_head)
    log_beta = jnp.minimum(z, 0.0) - LOG2E * jnp.log(1.0 + jnp.exp2(-jnp.abs(z)))
    log_om = log_beta - z
    if causal is not None:
        log_om = jnp.where(causal, log_om, 0.0)
    hi = log_om.astype(BF16)
    lo = (log_om - hi.astype(F32)).astype(BF16)
    tail = _dot(upper2, jnp.concatenate([hi, lo], axis=0))
    return log_beta, log_om, tail


def _sb_accumulate(log_beta, log_om, tail, carry, vt_blk, acc, causal):
    tail = tail + carry
    w = jnp.exp2(log_beta + tail)
    if causal is not None:
        w = jnp.where(causal, w, 0.0)
    acc = acc + _dot(vt_blk, w.astype(BF16))
    return tail[0:1, :] + log_om[0:1, :], acc


def _sb_kernel(q_ref, k_ref, vt_ref, upper_ref, o_ref):
    qi = pl.program_id(2)
    q = q_ref[...]
    lane = lax.broadcasted_iota(jnp.int32, (1, LANES), 1)
    q_heads = [jnp.where(lane < SB_DH, q, jnp.zeros_like(q)),
               jnp.where(lane >= SB_DH, q, jnp.zeros_like(q))]
    causal = (lax.broadcasted_iota(jnp.int32, (TK_SB, TQ_SB), 0)
              < lax.broadcasted_iota(jnp.int32, (TK_SB, TQ_SB), 1))
    upper2 = upper_ref[...]

    def load(j):
        start = pl.multiple_of(j * TK_SB, TK_SB)
        return k_ref[pl.ds(start, TK_SB), :], vt_ref[j]

    def stage(tiles, carries, accs):
        carries, accs = list(carries), list(accs)
        scored = [[_sb_scores(k_blk, q_heads[h], upper2, mask) for h in range(2)]
                  for k_blk, _, mask in tiles]
        for (_, vt_blk, mask), per_head in zip(tiles, scored):
            for h in range(2):
                carries[h], accs[h] = _sb_accumulate(
                    *per_head[h], carries[h], vt_blk, accs[h], mask)
        return carries, accs

    def finish(accs):
        vrow = lax.broadcasted_iota(jnp.int32, (LANES, 1), 0)
        out_t = jnp.where(vrow < SB_DH, accs[0], accs[1])
        o_ref[...] = out_t.T.astype(BF16)

    carry0 = [jnp.zeros((1, TQ_SB), F32)] * 2
    acc0 = [jnp.zeros((LANES, TQ_SB), F32)] * 2

    @pl.when(qi == 0)
    def _():
        _, accs = stage([(*load(0), causal)], carry0, acc0)
        finish(accs)

    @pl.when(qi > 0)
    def _():
        carries, accs = stage([(*load(qi), causal), (*load(qi - 1), None)], carry0, acc0)

        def largest(c):
            return jnp.max(jnp.maximum(c[0], c[1]))

        def cond(st):
            return (st[0] >= 0) & (st[1] > SB_SKIP_LOG2)

        def body(st):
            c, a = stage([(*load(st[0]), None)], st[2:4], st[4:6])
            return (st[0] - 1, largest(c), *c, *a)

        st = lax.while_loop(cond, body, (qi - 2, largest(carries), *carries, *accs))
        finish(st[4:6])


def _stick_breaking(main, vt, batch, seq):
    later = np.triu(np.ones((TK_SB, TK_SB)), 1)
    upper2 = jnp.asarray(np.concatenate([later, later], axis=1), BF16)
    n = batch * seq
    nq = seq // TQ_SB
    nk = seq // TK_SB
    pairs = SB_WIDTH // LANES
    return pl.pallas_call(
        _sb_kernel,
        grid=(batch, pairs, nq),
        in_specs=[
            pl.BlockSpec((None, TQ_SB, LANES), lambda b, p, i: (4, b * nq + i, p)),
            pl.BlockSpec((None, seq, LANES), lambda b, p, i: (5, b, p)),
            pl.BlockSpec((None, nk, LANES, TK_SB), lambda b, p, i: (b, 0, p, 0)),
            _resident((TK_SB, 2 * TK_SB)),
        ],
        out_specs=pl.BlockSpec((TQ_SB, LANES), lambda b, p, i: (b * nq + i, p)),
        out_shape=jax.ShapeDtypeStruct((n, SB_WIDTH), BF16),
        compiler_params=pltpu.CompilerParams(
            dimension_semantics=("arbitrary", "arbitrary", "arbitrary")),
        name="stick_breaking",
    )(main, main, vt, upper2)


def _kv_proj_kernel(mem_ref, g_ref, w_ref, kg_ref, k_ref, v_ref):
    m = _rms(mem_ref[...], g_ref[...]).astype(BF16)
    kv = _dot(m, w_ref[...])
    for hd in range(X_HEADS):
        cols = slice(hd * X_DH, (hd + 1) * X_DH)
        k_ref[:, cols] = _rms(kv[:, cols], kg_ref[:, cols]).astype(BF16)
    v_ref[...] = kv[:, D_MODEL:].astype(BF16)


def _kv_proj(mem2d, g_mem, w_xkv, xk_g, batch):
    out = jax.ShapeDtypeStruct((batch * MEM_LEN, D_MODEL), BF16)
    return pl.pallas_call(
        _kv_proj_kernel,
        grid=(batch,),
        in_specs=[
            pl.BlockSpec((MEM_LEN, D_MODEL), lambda b: (b, 0)),
            _resident((1, D_MODEL)),
            _resident((D_MODEL, 2 * D_MODEL)),
            _resident((1, D_MODEL)),
        ],
        out_specs=[pl.BlockSpec((MEM_LEN, D_MODEL), lambda b: (b, 0))] * 2,
        out_shape=[out, out],
        compiler_params=pltpu.CompilerParams(dimension_semantics=("arbitrary",)),
        name="kv_proj",
    )(mem2d, g_mem, w_xkv, xk_g)


def _mix_xattn_kernel(x_ref, ro_ref, so_ref, wout_ref, gx_ref, wq_ref, qg_ref,
                      xk_ref, xv_ref, wo_ref, o_ref):
    x1 = (x_ref[...] + _dot(ro_ref[...], wout_ref[:RET_WIDTH, :])
          + _dot(so_ref[...], wout_ref[RET_WIDTH:, :]))
    hx = _rms(x1, gx_ref[...]).astype(BF16)
    xq = _dot(hx, wq_ref[...])
    heads = []
    for hd in range(X_HEADS):
        cols = slice(hd * X_DH, (hd + 1) * X_DH)
        qh = (_rms(xq[:, cols], qg_ref[:, cols]) * (X_DH ** -0.5)).astype(BF16)
        s = _dot_nt(qh, xk_ref[:, cols])
        p = jnp.exp(s - jnp.max(s, axis=-1, keepdims=True))
        denom = jnp.sum(p, axis=-1, keepdims=True)
        heads.append((_dot(p.astype(BF16), xv_ref[:, cols]) / denom).astype(BF16))
    xo = jnp.concatenate(heads, axis=-1)
    o_ref[...] = x1 + _dot(xo, wo_ref[...])


def _mix_xattn(x2d, ro, so, w_out, g_xattn, w_xq, xq_g, xk, xv, w_xo, seq):
    n = x2d.shape[0]
    tiles_per_seq = seq // TM_POST
    row = lambda i: (i, 0)
    per_batch = lambda i: (i // tiles_per_seq, 0)
    return pl.pallas_call(
        _mix_xattn_kernel,
        grid=(n // TM_POST,),
        in_specs=[
            pl.BlockSpec((TM_POST, D_MODEL), row),
            pl.BlockSpec((TM_POST, RET_WIDTH), row),
            pl.BlockSpec((TM_POST, SB_WIDTH), row),
            _resident((D_MODEL, D_MODEL)),
            _resident((1, D_MODEL)),
            _resident((D_MODEL, D_MODEL)),
            _resident((1, D_MODEL)),
            pl.BlockSpec((MEM_LEN, D_MODEL), per_batch),
            pl.BlockSpec((MEM_LEN, D_MODEL), per_batch),
            _resident((D_MODEL, D_MODEL)),
        ],
        out_specs=pl.BlockSpec((TM_POST, D_MODEL), row),
        out_shape=jax.ShapeDtypeStruct((n, D_MODEL), F32),
        compiler_params=pltpu.CompilerParams(
            dimension_semantics=("arbitrary",), vmem_limit_bytes=48 * 1024 * 1024),
        name="mix_xattn",
    )(x2d, ro, so, w_out, g_xattn, w_xq, xq_g, xk, xv, w_xo)


def _mlp_kernel(x_ref, g_ref, wup_ref, wdown_ref, o_ref):
    x = x_ref[...]
    hm = _rms(x, g_ref[...]).astype(BF16)
    acc = x
    for j in range(D_FF // FF_CHUNK):
        cols = slice(j * FF_CHUNK, (j + 1) * FF_CHUNK)
        up = jnp.maximum(_dot(hm, wup_ref[:, cols]), 0.0)
        acc = acc + _dot((up * up).astype(BF16), wdown_ref[cols, :])
    o_ref[...] = acc


def _mlp(x2d, g_mlp, w_up, w_down):
    n = x2d.shape[0]
    row = lambda i: (i, 0)
    return pl.pallas_call(
        _mlp_kernel,
        grid=(n // TM_POST,),
        in_specs=[
            pl.BlockSpec((TM_POST, D_MODEL), row),
            _resident((1, D_MODEL)),
            _resident((D_MODEL, D_FF)),
            _resident((D_FF, D_MODEL)),
        ],
        out_specs=pl.BlockSpec((TM_POST, D_MODEL), row),
        out_shape=jax.ShapeDtypeStruct((n, D_MODEL), F32),
        compiler_params=pltpu.CompilerParams(
            dimension_semantics=("arbitrary",), vmem_limit_bytes=48 * 1024 * 1024),
        name="mlp",
    )(x2d, g_mlp, w_up, w_down)


def kernel(x, mem, positions, g_mix, w_in, ret_gn_g, sb_q_g, sb_k_g, w_out, g_xattn, g_mem,
           w_xq, w_xkv, xq_g, xk_g, w_xo, g_mlp, w_up, w_down):
    batch, seq, _ = x.shape
    n = batch * seq
    depth = g_mix.shape[0]

    half = np.arange(LANES) % (RET_DK // 2)
    freq = jnp.asarray((ROPE_BASE ** (-(2.0 * half) / RET_DK)).reshape(1, LANES), F32)
    head_of = np.arange(SB_WIDTH) // SB_DH
    seg = jnp.asarray(head_of[:, None] == head_of[None, :], BF16)
    pos_b = jnp.broadcast_to(positions.astype(F32).reshape(n, 1), (n, LANES))

    x2d = x.reshape(n, D_MODEL)
    mem2d = mem.reshape(batch * MEM_LEN, D_MODEL)
    for layer in range(depth):
        main, vt = _in_proj(
            x2d, pos_b, freq, g_mix[layer].reshape(1, D_MODEL), w_in[layer].astype(BF16),
            sb_q_g[layer].reshape(1, SB_WIDTH), sb_k_g[layer].reshape(1, SB_WIDTH), seg,
            batch, seq)
        ro = _retention(main, ret_gn_g[layer].reshape(1, RET_WIDTH), batch, seq)
        so = _stick_breaking(main, vt, batch, seq)
        xk, xv = _kv_proj(mem2d, g_mem[layer].reshape(1, D_MODEL), w_xkv[layer].astype(BF16),
                          xk_g[layer].reshape(1, D_MODEL), batch)
        x2d = _mix_xattn(x2d, ro, so, w_out[layer].astype(BF16),
                         g_xattn[layer].reshape(1, D_MODEL), w_xq[layer].astype(BF16),
                         xq_g[layer].reshape(1, D_MODEL), xk, xv, w_xo[layer].astype(BF16), seq)
        x2d = _mlp(x2d, g_mlp[layer].reshape(1, D_MODEL), w_up[layer].astype(BF16),
                   w_down[layer].astype(BF16))
    return x2d.reshape(batch, seq, D_MODEL)
```
